```python
import jax, jax.numpy as jnp
from jax import lax
import numpy as np


D_MODEL = 1024
BATCH = 2
SEQ = 8192
DEPTH = 1

EPS = 1e-6
CONV_WIDTH = 3
D_CONV = D_MODEL
GLA_HEADS = 4
GLA_DK = D_MODEL // 2
GLA_DV = D_MODEL
GLA_HEAD_DK = GLA_DK // GLA_HEADS
GLA_HEAD_DV = GLA_DV // GLA_HEADS
GLA_LOWRANK = 16
GLA_TAU = 16.0
GLA_CHUNK = 64
PEER_HEADS = 8
PEER_NKEYS = 128
PEER_EXPERTS = PEER_NKEYS ** 2
PEER_TOPK = 16
PEER_QDIM = 256
PEER_TOK_BLOCK = 128
IN_SIZES = (D_CONV, D_CONV, D_CONV,
            GLA_DK, GLA_DK, GLA_DV, GLA_DV,
            GLA_LOWRANK, GLA_LOWRANK,
            D_MODEL, D_MODEL)
IN_SPLITS = tuple(int(s) for s in np.cumsum(IN_SIZES)[:-1])
W_IN_COLS = int(sum(IN_SIZES))

kernel_name = 'hybrid_shortconv_gla_peer_encoder_block'


def _rmsnorm(x, g):
    xf = x.astype(jnp.float32)
    xf = xf * lax.rsqrt(jnp.mean(xf * xf, axis=-1, keepdims=True) + EPS)
    return (xf * g.astype(jnp.float32)).astype(x.dtype)


def _centred_depthwise_conv(u, w, b):
    pad = CONV_WIDTH // 2
    T = u.shape[1]
    up = jnp.pad(u, ((0, 0), (pad, pad), (0, 0)))
    y = b
    for j in range(CONV_WIDTH):
        y = y + up[:, j:j + T] * w[j]
    return y


def _gla_one_direction(q, k, v, log_a):
    Bsz, T, H, dk = q.shape
    dv = v.shape[-1]
    n = T // GLA_CHUNK
    f32 = jnp.float32

    def chunks(t):
        return t.astype(f32).reshape(Bsz, n, GLA_CHUNK, H, t.shape[-1])

    q, k, v, la = chunks(q), chunks(k), chunks(v), chunks(log_a)
    b = jnp.cumsum(la, axis=2)
    b_last = b[:, :, -1]
    b_ref = b[:, :, GLA_CHUNK // 2][:, :, None]
    q_in = q * jnp.exp(b - b_ref)
    k_in = k * jnp.exp(b_ref - b)
    mask = jnp.tril(jnp.ones((GLA_CHUNK, GLA_CHUNK), dtype=bool))
    scores = jnp.einsum('bnihd,bnjhd->bnhij', q_in, k_in)
    scores = jnp.where(mask, scores, 0.0)
    o_intra = jnp.einsum('bnhij,bnjhe->bnihe', scores, v)
    k_to_end = k * jnp.exp(b_last[:, :, None] - b)
    u = jnp.einsum('bnjhd,bnjhe->bnhde', k_to_end, v)
    decay = jnp.exp(b_last)

    def step(S, inp):
        dec, uc = inp
        return dec[..., None] * S + uc, S

    S0 = jnp.zeros((Bsz, H, dk, dv), f32)
    _, S_prev = lax.scan(step, S0, (jnp.moveaxis(decay, 1, 0), jnp.moveaxis(u, 1, 0)))
    S_prev = jnp.moveaxis(S_prev, 0, 1)
    o_inter = jnp.einsum('bnihd,bnhde->bnihe', q * jnp.exp(b), S_prev)
    return (o_intra + o_inter).reshape(Bsz, T, H, dv)


def _bidirectional_gla(q, k, v, la_f, la_b):
    o_f = _gla_one_direction(q, k, v, la_f)

    def flip(t):
        return jnp.flip(t, axis=1)

    o_b = flip(_gla_one_direction(flip(q), flip(k), flip(v), flip(la_b)))
    return o_f + o_b


def _mixer(xn, w_in, conv_w, conv_b, wa_proj, decay_up_f, decay_bias_f, decay_up_b,
           decay_bias_b, gla_norm_g, wb_proj, gate_bias, w_out):
    Bsz, T, _ = xn.shape
    h = xn @ w_in
    xa, ba, ca, q, k, v, r, zf, zb, ga, gb = jnp.split(h, IN_SPLITS, axis=-1)

    ya = ba * _centred_depthwise_conv(ca * xa, conv_w, conv_b)
    out_a = ya @ wa_proj

    def heads(t, d):
        return t.reshape(Bsz, T, GLA_HEADS, d)

    qh = heads(q, GLA_HEAD_DK) * (GLA_HEAD_DK ** -0.5)
    kh = heads(k, GLA_HEAD_DK)
    vh = heads(v, GLA_HEAD_DV)
    la_f = jax.nn.log_sigmoid((zf @ decay_up_f + decay_bias_f).astype(jnp.float32)) / GLA_TAU
    la_b = jax.nn.log_sigmoid((zb @ decay_up_b + decay_bias_b).astype(jnp.float32)) / GLA_TAU
    o = _bidirectional_gla(qh, kh, vh, heads(la_f, GLA_HEAD_DK), heads(la_b, GLA_HEAD_DK))
    o = _rmsnorm(o, gla_norm_g.reshape(GLA_HEADS, GLA_HEAD_DV))
    o = o.reshape(Bsz, T, GLA_DV).astype(xn.dtype) * jax.nn.silu(r)
    out_b = o @ wb_proj

    gate_a = jax.nn.sigmoid(ga + gate_bias[0])
    gate_b = jax.nn.sigmoid(gb + gate_bias[1])
    return (gate_a * out_a + gate_b * out_b) @ w_out


def _peer(xn, wq, keys, u_tab, v_tab):
    Bsz, T, D = xn.shape
    BT = Bsz * T
    xt = xn.reshape(BT, D)
    q = (xt @ wq).reshape(BT, PEER_HEADS, 2, PEER_QDIM // 2)
    s = jnp.einsum('thpd,hpkd->thpk', q, keys).astype(jnp.float32)
    top_s, top_i = lax.top_k(s, PEER_TOPK)
    cand = (top_s[:, :, 0, :, None] + top_s[:, :, 1, None, :]).reshape(BT, PEER_HEADS, PEER_TOPK * PEER_TOPK)
    best_s, best_c = lax.top_k(cand, PEER_TOPK)
    i1 = jnp.take_along_axis(top_i[:, :, 0], best_c // PEER_TOPK, axis=-1)
    i2 = jnp.take_along_axis(top_i[:, :, 1], best_c % PEER_TOPK, axis=-1)
    experts = (i1 * PEER_NKEYS + i2).reshape(BT, PEER_HEADS * PEER_TOPK)
    gates = jax.nn.softmax(best_s, axis=-1).reshape(BT, PEER_HEADS * PEER_TOPK).astype(xn.dtype)
    nb = BT // PEER_TOK_BLOCK

    def block(args):
        xb, eb, gbk = args
        hidden = jax.nn.gelu(jnp.einsum('td,tkd->tk', xb, u_tab[eb]), approximate=False)
        return jnp.einsum('tk,tkd->td', gbk * hidden, v_tab[eb])

    out = lax.map(block, (xt.reshape(nb, PEER_TOK_BLOCK, D),
                          experts.reshape(nb, PEER_TOK_BLOCK, -1),
                          gates.reshape(nb, PEER_TOK_BLOCK, -1)))
    return out.reshape(Bsz, T, D)


def setup_inputs(seed: int = 0) -> dict:
    key = jax.random.key(seed)
    ks = jax.random.split(key, 24)

    def nrm(k, shape, scale):
        return jax.random.normal(k, shape, jnp.float32) * scale

    L = DEPTH
    return {
        'x': nrm(ks[0], (BATCH, SEQ, D_MODEL), 1.0),
        'norm1_g': 1.0 + nrm(ks[1], (L, D_MODEL), 0.02),
        'w_in': nrm(ks[2], (L, D_MODEL, W_IN_COLS), D_MODEL ** -0.5),
        'conv_w': nrm(ks[3], (L, CONV_WIDTH, D_CONV), CONV_WIDTH ** -0.5),
        'conv_b': nrm(ks[4], (L, D_CONV), 0.01),
        'wa_proj': nrm(ks[5], (L, D_CONV, D_MODEL), D_CONV ** -0.5),
        'decay_up_f': nrm(ks[6], (L, GLA_LOWRANK, GLA_DK), GLA_LOWRANK ** -0.5),
        'decay_bias_f': 1.0 + nrm(ks[7], (L, GLA_DK), 0.5),
        'decay_up_b': nrm(ks[8], (L, GLA_LOWRANK, GLA_DK), GLA_LOWRANK ** -0.5),
        'decay_bias_b': 1.0 + nrm(ks[9], (L, GLA_DK), 0.5),
        'gla_norm_g': 1.0 + nrm(ks[10], (L, GLA_DV), 0.02),
        'wb_proj': nrm(ks[11], (L, GLA_DV, D_MODEL), GLA_DV ** -0.5),
        'gate_bias': nrm(ks[12], (L, 2, D_MODEL), 0.01),
        'w_out': nrm(ks[13], (L, D_MODEL, D_MODEL), D_MODEL ** -0.5),
        'norm2_g': 1.0 + nrm(ks[14], (L, D_MODEL), 0.02),
        'peer_wq': nrm(ks[15], (L, D_MODEL, PEER_HEADS * PEER_QDIM), D_MODEL ** -0.5),
        'peer_keys': nrm(ks[16], (L, PEER_HEADS, 2, PEER_NKEYS, PEER_QDIM // 2), (PEER_QDIM // 2) ** -0.5),
        'peer_u': nrm(ks[17], (L, PEER_EXPERTS, D_MODEL), D_MODEL ** -0.5),
        'peer_v': nrm(ks[18], (L, PEER_EXPERTS, D_MODEL), (PEER_HEADS * PEER_TOPK) ** -0.5),
        'final_norm_g': 1.0 + nrm(ks[19], (D_MODEL,), 0.02),
    }


def reference(x, norm1_g, w_in, conv_w, conv_b, wa_proj, decay_up_f, decay_bias_f,
              decay_up_b, decay_bias_b, gla_norm_g, wb_proj, gate_bias, w_out, norm2_g,
              peer_wq, peer_keys, peer_u, peer_v, final_norm_g):
    for l in range(DEPTH):
        xn = _rmsnorm(x, norm1_g[l])
        x = x + _mixer(xn, w_in[l], conv_w[l], conv_b[l], wa_proj[l], decay_up_f[l],
                       decay_bias_f[l], decay_up_b[l], decay_bias_b[l], gla_norm_g[l],
                       wb_proj[l], gate_bias[l], w_out[l])
        xn = _rmsnorm(x, norm2_g[l])
        x = x + _peer(xn, peer_wq[l], peer_keys[l], peer_u[l], peer_v[l])
    return _rmsnorm(x, final_norm_g)
```

```python
import functools

import jax
import jax.numpy as jnp
import numpy as np
from jax import lax
from jax.experimental import pallas as pl
from jax.experimental.pallas import tpu as pltpu

F32 = jnp.float32
BF16 = jnp.bfloat16

EPS = 1e-6
CONV_WIDTH = 3
GLA_HEADS = 4
GLA_LOWRANK = 16
GLA_TAU = 16.0
GLA_CHUNK = 64
PEER_HEADS = 8
PEER_NKEYS = 128
PEER_TOPK = 16

VMEM_LIMIT_BYTES = 56 * 1024 * 1024
LANES = 128
SUBLANES = 8

NEG_INF = float("-inf")


def _cparams(sem):
    return pltpu.CompilerParams(dimension_semantics=sem, vmem_limit_bytes=VMEM_LIMIT_BYTES)


def _rms(x, g):
    return x * lax.rsqrt(jnp.mean(x * x, axis=-1, keepdims=True) + EPS) * g


def _inproj_kernel(x_ref, g_ref, w_ref, wz_ref, h_ref, z_ref, xn_ref):
    @pl.when(pl.program_id(1) == 0)
    def _():
        xn_ref[...] = _rms(x_ref[...], g_ref[...]).astype(BF16)
        z_ref[...] = jnp.dot(xn_ref[...], wz_ref[...], preferred_element_type=F32)

    h_ref[...] = jnp.dot(xn_ref[...], w_ref[...], preferred_element_type=F32).astype(BF16)


def _inproj(x, g, w, wz, tm, tn):
    T, D = x.shape
    N = w.shape[1]
    return pl.pallas_call(
        _inproj_kernel,
        grid=(T // tm, N // tn),
        in_specs=[
            pl.BlockSpec((tm, D), lambda i, j: (i, 0)),
            pl.BlockSpec((1, D), lambda i, j: (0, 0)),
            pl.BlockSpec((D, tn), lambda i, j: (0, j)),
            pl.BlockSpec((D, LANES), lambda i, j: (0, 0)),
        ],
        out_specs=[
            pl.BlockSpec((tm, tn), lambda i, j: (i, j)),
            pl.BlockSpec((tm, LANES), lambda i, j: (i, 0)),
        ],
        out_shape=[
            jax.ShapeDtypeStruct((T, N), BF16),
            jax.ShapeDtypeStruct((T, LANES), F32),
        ],
        scratch_shapes=[pltpu.VMEM((tm, D), BF16)],
        compiler_params=_cparams(("parallel", "arbitrary")),
        name="inproj",
    )(x, g, w, wz)


def _conv_kernel(tiles_per_seq, xa_ref, ba_ref, ca_ref, ga_ref, xap_ref, cap_ref, xan_ref, can_ref,
                 cw_ref, cb_ref, wa_ref, gbias_ref, o_ref):
    i = pl.program_id(0)
    tm = xa_ref.shape[0]
    u = ca_ref[...].astype(F32) * xa_ref[...].astype(F32)
    has_prev = (i % tiles_per_seq != 0).astype(F32)
    has_next = (i % tiles_per_seq != tiles_per_seq - 1).astype(F32)
    up_row = (cap_ref[...].astype(F32) * xap_ref[...].astype(F32))[SUBLANES - 1:SUBLANES, :] * has_prev
    un_row = (can_ref[...].astype(F32) * xan_ref[...].astype(F32))[0:1, :] * has_next
    row = lax.broadcasted_iota(jnp.int32, u.shape, 0)
    u_prev = jnp.where(row == 0, up_row, pltpu.roll(u, 1, axis=0))
    u_next = jnp.where(row == tm - 1, un_row, pltpu.roll(u, tm - 1, axis=0))
    cw = cw_ref[...]
    y = cb_ref[...] + u_prev * cw[0:1, :] + u * cw[1:2, :] + u_next * cw[2:3, :]
    ya = (ba_ref[...].astype(F32) * y).astype(BF16)
    out_a = jnp.dot(ya, wa_ref[...], preferred_element_type=F32)
    gate = jax.nn.sigmoid(ga_ref[...].astype(F32) + gbias_ref[...])
    o_ref[...] = gate * out_a


def _conv_branch(h, conv_w, conv_b, wa, gbias_a, seq, tm, col):
    T = h.shape[0]
    D = conv_w.shape[1]
    nb = D // D
    del nb
    tiles_per_seq = seq // tm
    r8 = tm // SUBLANES
    last8 = T // SUBLANES - 1

    def cblk(c):
        return pl.BlockSpec((tm, D), lambda i, c=c: (i, c))

    def prev_blk(c):
        return pl.BlockSpec((SUBLANES, D), lambda i, c=c: (jnp.maximum(i * r8 - 1, 0), c))

    def next_blk(c):
        return pl.BlockSpec((SUBLANES, D), lambda i, c=c: (jnp.minimum((i + 1) * r8, last8), c))

    def full(shape):
        return pl.BlockSpec(shape, lambda i: (0,) * len(shape))

    return pl.pallas_call(
        functools.partial(_conv_kernel, tiles_per_seq),
        grid=(T // tm,),
        in_specs=[
            cblk(col["xa"]), cblk(col["ba"]), cblk(col["ca"]), cblk(col["ga"]),
            prev_blk(col["xa"]), prev_blk(col["ca"]), next_blk(col["xa"]), next_blk(col["ca"]),
            full((CONV_WIDTH, D)), full((1, D)), full((D, D)), full((1, D)),
        ],
        out_specs=pl.BlockSpec((tm, D), lambda i: (i, 0)),
        out_shape=jax.ShapeDtypeStruct((T, D), F32),
        compiler_params=_cparams(("parallel",)),
        name="conv_branch",
    )(h, h, h, h, h, h, h, h, conv_w, conv_b, wa, gbias_a)


def _log_sigmoid(x):
    return jnp.minimum(x, 0.0) - jnp.log1p(jnp.exp(-jnp.abs(x)))


def _gla_chunk(q, k, v, la, st, tri, reverse):
    c = q.shape[0]
    b = jnp.dot(tri, la, preferred_element_type=F32, precision=lax.Precision.HIGHEST)
    if reverse:
        b_end = b[0:1, :]
        b_mid = b[c // 2 - 1:c // 2, :]
    else:
        b_end = b[c - 1:c, :]
        b_mid = b[c // 2:c // 2 + 1, :]
    q_in = (q * jnp.exp(b - b_mid)).astype(BF16)
    k_in = (k * jnp.exp(b_mid - b)).astype(BF16)
    scores = lax.dot_general(q_in, k_in, (((1,), (1,)), ((), ())), preferred_element_type=F32)
    ri = lax.broadcasted_iota(jnp.int32, scores.shape, 0)
    ci = lax.broadcasted_iota(jnp.int32, scores.shape, 1)
    keep = (ci >= ri) if reverse else (ci <= ri)
    scores = jnp.where(keep, scores, 0.0).astype(BF16)
    vb = v.astype(BF16)
    o = jnp.dot(scores, vb, preferred_element_type=F32)
    q_out = (q * jnp.exp(b)).astype(BF16)
    o = o + lax.dot_general(q_out, st.astype(BF16), (((1,), (1,)), ((), ())),
                            preferred_element_type=F32)
    k_end = (k * jnp.exp(b_end - b)).astype(BF16)
    ut = lax.dot_general(vb, k_end, (((0,), (0,)), ((), ())), preferred_element_type=F32)
    st_new = st * jnp.exp(b_end) + ut
    return o, st_new


def _gla_kernel(qf_ref, kf_ref, vf_ref, zf_ref, qb_ref, kb_ref, vb_ref, zb_ref,
                upf_ref, bf_ref, upb_ref, bb_ref, of_ref, ob_ref,
                sf_ref, sb_ref, laf_ref, lab_ref):
    tb = qf_ref.shape[0]
    dk = qf_ref.shape[1] // GLA_HEADS
    dv = vf_ref.shape[1] // GLA_HEADS
    nchunk = tb // GLA_CHUNK
    scale = dk ** -0.5

    @pl.when(pl.program_id(1) == 0)
    def _():
        sf_ref[...] = jnp.zeros_like(sf_ref)
        sb_ref[...] = jnp.zeros_like(sb_ref)

    hi = lax.Precision.HIGHEST
    laf_ref[...] = _log_sigmoid(
        jnp.dot(zf_ref[...], upf_ref[...], preferred_element_type=F32, precision=hi) + bf_ref[...]
    ) * (1.0 / GLA_TAU)
    lab_ref[...] = _log_sigmoid(
        jnp.dot(zb_ref[...], upb_ref[...], preferred_element_type=F32, precision=hi) + bb_ref[...]
    ) * (1.0 / GLA_TAU)

    ri = lax.broadcasted_iota(jnp.int32, (GLA_CHUNK, GLA_CHUNK), 0)
    ci = lax.broadcasted_iota(jnp.int32, (GLA_CHUNK, GLA_CHUNK), 1)
    tril = (ci <= ri).astype(F32)
    triu = (ci >= ri).astype(F32)

    def body(c, carry):
        rf = pl.ds(pl.multiple_of(c * GLA_CHUNK, GLA_CHUNK), GLA_CHUNK)
        rb = pl.ds(pl.multiple_of((nchunk - 1 - c) * GLA_CHUNK, GLA_CHUNK), GLA_CHUNK)
        for h in range(GLA_HEADS):
            ck = slice(h * dk, (h + 1) * dk)
            cv = slice(h * dv, (h + 1) * dv)
            o, st = _gla_chunk(qf_ref[rf, ck].astype(F32) * scale, kf_ref[rf, ck].astype(F32),
                               vf_ref[rf, cv].astype(F32), laf_ref[rf, ck], sf_ref[h], tril, False)
            of_ref[rf, cv] = o
            sf_ref[h] = st
            o, st = _gla_chunk(qb_ref[rb, ck].astype(F32) * scale, kb_ref[rb, ck].astype(F32),
                               vb_ref[rb, cv].astype(F32), lab_ref[rb, ck], sb_ref[h], triu, True)
            ob_ref[rb, cv] = o
            sb_ref[h] = st
        return carry

    lax.fori_loop(0, nchunk, body, 0)


def _gla(h, z, upf, bf, upb, bb, batch, seq, tb, col, dk_total, dv_total):
    T = h.shape[0]
    nblk = seq // tb

    def fwd(w, c):
        return pl.BlockSpec((tb, w), lambda b, i, c=c: (b * nblk + i, c))

    def bwd(w, c):
        return pl.BlockSpec((tb, w), lambda b, i, c=c: (b * nblk + nblk - 1 - i, c))

    def full(shape):
        return pl.BlockSpec(shape, lambda b, i: (0,) * len(shape))

    dk = dk_total // GLA_HEADS
    dv = dv_total // GLA_HEADS
    return pl.pallas_call(
        _gla_kernel,
        grid=(batch, nblk),
        in_specs=[
            fwd(dk_total, col["q"]), fwd(dk_total, col["k"]), fwd(dv_total, col["v"]), fwd(LANES, 0),
            bwd(dk_total, col["q"]), bwd(dk_total, col["k"]), bwd(dv_total, col["v"]), bwd(LANES, 0),
            full((LANES, dk_total)), full((1, dk_total)), full((LANES, dk_total)), full((1, dk_total)),
        ],
        out_specs=[fwd(dv_total, 0), bwd(dv_total, 0)],
        out_shape=[jax.ShapeDtypeStruct((T, dv_total), F32)] * 2,
        scratch_shapes=[
            pltpu.VMEM((GLA_HEADS, dv, dk), F32),
            pltpu.VMEM((GLA_HEADS, dv, dk), F32),
            pltpu.VMEM((tb, dk_total), F32),
            pltpu.VMEM((tb, dk_total), F32),
        ],
        compiler_params=_cparams(("parallel", "arbitrary")),
        name="gla",
    )(h, h, h, z, h, h, h, z, upf, bf, upb, bb)


def _post_kernel(x_ref, of_ref, ob_ref, r_ref, gb_ref, ma_ref, gn_ref, wb_ref, gbias_ref, wo_ref,
                 g2_ref, wq_ref, keys_ref, x2_ref, xn2_ref, st_ref):
    dv = of_ref.shape[1] // GLA_HEADS
    o = of_ref[...] + ob_ref[...]
    gn = gn_ref[...]
    parts = []
    for h in range(GLA_HEADS):
        cs = slice(h * dv, (h + 1) * dv)
        parts.append(_rms(o[:, cs], gn[:, cs]))
    o = jnp.concatenate(parts, axis=-1)
    r = r_ref[...].astype(F32)
    o = (o * (r * jax.nn.sigmoid(r))).astype(BF16)
    out_b = jnp.dot(o, wb_ref[...], preferred_element_type=F32)
    gate_b = jax.nn.sigmoid(gb_ref[...].astype(F32) + gbias_ref[...])
    m = (ma_ref[...] + gate_b * out_b).astype(BF16)
    x2 = x_ref[...] + jnp.dot(m, wo_ref[...], preferred_element_type=F32)
    x2_ref[...] = x2
    xn2 = _rms(x2, g2_ref[...]).astype(BF16)
    xn2_ref[...] = xn2
    q = jnp.dot(xn2, wq_ref[...], preferred_element_type=F32).astype(BF16)
    qd = keys_ref.shape[2]
    for hp in range(keys_ref.shape[0]):
        st_ref[hp] = lax.dot_general(keys_ref[hp], q[:, hp * qd:(hp + 1) * qd],
                                     (((1,), (1,)), ((), ())), preferred_element_type=F32)


def _post(x, of, ob, h, ma, gn, wb, gbias_b, wo, g2, wq, keys, tm, col):
    T, D = x.shape
    nhp, nk, qd = keys.shape

    def tok(w, c=0):
        return pl.BlockSpec((tm, w), lambda i, c=c: (i, c))

    def full(shape):
        return pl.BlockSpec(shape, lambda i: (0,) * len(shape))

    return pl.pallas_call(
        _post_kernel,
        grid=(T // tm,),
        in_specs=[
            tok(D), tok(D), tok(D), tok(D, col["r"]), tok(D, col["gb"]), tok(D),
            full((1, D)), full((D, D)), full((1, D)), full((D, D)),
            full((1, D)), full(wq.shape), full(keys.shape),
        ],
        out_specs=[tok(D), tok(D), pl.BlockSpec((nhp, nk, tm), lambda i: (0, 0, i))],
        out_shape=[
            jax.ShapeDtypeStruct((T, D), F32),
            jax.ShapeDtypeStruct((T, D), BF16),
            jax.ShapeDtypeStruct((nhp, nk, T), F32),
        ],
        compiler_params=_cparams(("parallel",)),
        name="post",
    )(x, of, ob, h, h, ma, gn, wb, gbias_b, wo, g2, wq, keys)


def _cand_rows():
    K = PEER_TOPK
    pieces = []
    for r1 in range(K // 2):
        lim = K // (r1 + 1)
        for start in range(0, lim, SUBLANES):
            pieces.append(("row", r1, start, min(lim - start, SUBLANES)))
    pieces.append(("col", K // 2, 0, SUBLANES))
    return pieces


def _top_ranks(a, k):
    n = a.shape[0]
    row = lax.broadcasted_iota(jnp.int32, a.shape, 0).astype(F32)
    rank = jnp.full(a.shape, float(k), F32)
    tops = []
    for r in range(k):
        m = jnp.max(a, axis=0, keepdims=True)
        first = jnp.min(jnp.where(a == m, row, float(n)), axis=0, keepdims=True)
        sel = row == first
        rank = jnp.where(sel, float(r), rank)
        a = jnp.where(sel, NEG_INF, a)
        tops.append(m)
    return rank, tops


def _stack_rows(rows, sub):
    out = jnp.broadcast_to(rows[0], sub.shape)
    for d in range(1, len(rows)):
        out = jnp.where(sub == d, rows[d], out)
    return out


def _topk_kernel(s_ref, r2_ref, e2_ref, n1_ref, e1_ref):
    K = PEER_TOPK
    L = s_ref.shape[2]
    pieces = _cand_rows()
    sub = lax.broadcasted_iota(jnp.int32, (SUBLANES, L), 0)
    for h in range(PEER_HEADS):
        s1 = s_ref[2 * h]
        s2 = s_ref[2 * h + 1]
        rank1, top1 = _top_ranks(s1, K)
        rank2, top2 = _top_ranks(s2, K)
        t1_hi = _stack_rows(top1[K // 2:], sub)
        t2_lo = _stack_rows(top2[:SUBLANES], sub)
        t2_hi = _stack_rows(top2[SUBLANES:], sub)
        cands, flats = [], []
        for kind, r1, start, cnt in pieces:
            if kind == "row":
                c = top1[r1] + (t2_lo if start == 0 else t2_hi)
                c = jnp.where(sub < cnt, c, NEG_INF)
                f = (r1 * K + start + sub).astype(F32)
            else:
                c = t1_hi + top2[0]
                f = ((r1 + sub) * K).astype(F32)
            cands.append(c)
            flats.append(f)
        cand = jnp.concatenate(cands, axis=0)
        flat = jnp.concatenate(flats, axis=0)
        cmax = top1[0] + top2[0]
        work = cand
        taken = jnp.zeros(cand.shape, F32)
        for _ in range(K):
            m = jnp.max(work, axis=0, keepdims=True)
            first = jnp.min(jnp.where(work == m, flat, float(K * K)), axis=0, keepdims=True)
            sel = flat == first
            taken = jnp.where(sel, 1.0, taken)
            work = jnp.where(sel, NEG_INF, work)
        z = jnp.sum(jnp.where(taken > 0.0, jnp.exp(cand - cmax), 0.0), axis=0, keepdims=True)
        n1 = jnp.zeros(s1.shape, F32)
        off = 0
        counts = {}
        for kind, r1, start, cnt in pieces:
            blk = taken[off:off + SUBLANES]
            off += SUBLANES
            if kind == "row":
                counts[r1] = counts.get(r1, 0.0) + jnp.sum(blk, axis=0, keepdims=True)
            else:
                for d in range(SUBLANES):
                    counts[r1 + d] = blk[d:d + 1]
        for r1 in range(K):
            n1 = jnp.where(rank1 == float(r1), counts[r1], n1)
        r2_ref[h] = rank2
        n1_ref[h] = n1
        e1_ref[h] = jnp.exp(s1 - top1[0])
        e2_ref[h] = jnp.exp(s2 - top2[0]) / z


def _topk(st, tn):
    nhp, nk, T = st.shape
    out = jax.ShapeDtypeStruct((PEER_HEADS, nk, T), F32)
    spec = pl.BlockSpec((PEER_HEADS, nk, tn), lambda i: (0, 0, i))
    return pl.pallas_call(
        _topk_kernel,
        grid=(T // tn,),
        in_specs=[pl.BlockSpec((nhp, nk, tn), lambda i: (0, 0, i))],
        out_specs=[spec] * 4,
        out_shape=[out] * 4,
        compiler_params=_cparams(("parallel",)),
        name="topk",
    )(st)


def _gelu(x):
    return 0.5 * x * (1.0 + lax.erf(x * (2.0 ** -0.5)))


def _peer_kernel(xn_ref, x2_ref, u_ref, vt_ref, r2_ref, e2_ref, n1_ref, e1_ref, gf_ref, o_ref,
                 acc_ref, hid_ref, w_ref):
    j = pl.program_id(1)
    nk = r2_ref.shape[1]
    per_blk = u_ref.shape[0] // nk

    @pl.when(j == 0)
    def _():
        acc_ref[...] = jnp.zeros_like(acc_ref)

    hid_ref[...] = lax.dot_general(u_ref[...], xn_ref[...], (((1,), (1,)), ((), ())),
                                   preferred_element_type=F32)

    def body(a, carry):
        i1 = j * per_blk + a
        rows = pl.ds(pl.multiple_of(a * nk, nk), nk)
        g = None
        for h in range(PEER_HEADS):
            n1 = n1_ref[h, pl.ds(i1, 1), :]
            e1 = e1_ref[h, pl.ds(i1, 1), :]
            t = jnp.where(r2_ref[h] < n1, e2_ref[h], 0.0) * e1
            g = t if g is None else g + t
        hid = hid_ref[rows, :]
        w_ref[rows, :] = (_gelu(hid) * g).astype(BF16)
        return carry

    lax.fori_loop(0, per_blk, body, 0)
    acc_ref[...] += jnp.dot(vt_ref[...], w_ref[...], preferred_element_type=F32)

    @pl.when(j == pl.num_programs(1) - 1)
    def _():
        x3 = x2_ref[...] + acc_ref[...].T
        o_ref[...] = _rms(x3, gf_ref[...])


def _peer(xn2, x2, u, vt, r2, e2, n1, e1, gf, tn, eb):
    T, D = x2.shape
    E = u.shape[0]
    nk = r2.shape[1]
    tab = pl.BlockSpec((PEER_HEADS, nk, tn), lambda i, j: (0, 0, i))
    return pl.pallas_call(
        _peer_kernel,
        grid=(T // tn, E // eb),
        in_specs=[
            pl.BlockSpec((tn, D), lambda i, j: (i, 0)),
            pl.BlockSpec((tn, D), lambda i, j: (i, 0)),
            pl.BlockSpec((eb, D), lambda i, j: (j, 0)),
            pl.BlockSpec((D, eb), lambda i, j: (0, j)),
            tab, tab, tab, tab,
            pl.BlockSpec((1, D), lambda i, j: (0, 0)),
        ],
        out_specs=pl.BlockSpec((tn, D), lambda i, j: (i, 0)),
        out_shape=jax.ShapeDtypeStruct((T, D), F32),
        scratch_shapes=[
            pltpu.VMEM((D, tn), F32),
            pltpu.VMEM((eb, tn), F32),
            pltpu.VMEM((eb, tn), BF16),
        ],
        compiler_params=_cparams(("parallel", "arbitrary")),
        name="peer",
    )(xn2, x2, u, vt, r2, e2, n1, e1, gf)


def _layer(x, norm1_g, w_in, conv_w, conv_b, wa_proj, decay_up_f, decay_bias_f, decay_up_b,
           decay_bias_b, gla_norm_g, wb_proj, gate_bias, w_out, norm2_g, peer_wq, peer_keys,
           peer_u, peer_v, out_g, batch, seq):
    T, D = x.shape
    dk_total = decay_up_f.shape[1]
    dv_total = wb_proj.shape[0]
    d_conv = conv_w.shape[1]
    assert d_conv == D and dv_total == D and 2 * dk_total == D

    sizes = (d_conv, d_conv, d_conv, dk_total, dk_total, dv_total, dv_total,
             GLA_LOWRANK, GLA_LOWRANK, D, D)
    offs = np.concatenate([[0], np.cumsum(sizes)])
    names = ("xa", "ba", "ca", "q", "k", "v", "r", "zf", "zb", "ga", "gb")
    seg = {n: (int(offs[i]), int(offs[i + 1])) for i, n in enumerate(names)}
    main = ("xa", "ba", "ca", "q", "k", "v", "r", "ga", "gb")
    w_main = jnp.concatenate([w_in[:, seg[n][0]:seg[n][1]] for n in main], axis=1).astype(BF16)
    wz = jnp.concatenate([w_in[:, seg["zf"][0]:seg["zb"][1]],
                          jnp.zeros((D, LANES - 2 * GLA_LOWRANK), w_in.dtype)], axis=1).astype(BF16)
    col, o = {}, 0
    for n in main:
        wdt = seg[n][1] - seg[n][0]
        col[n] = o // wdt
        o += wdt

    upf = jnp.zeros((LANES, dk_total), F32).at[:GLA_LOWRANK].set(decay_up_f)
    upb = jnp.zeros((LANES, dk_total), F32).at[GLA_LOWRANK:2 * GLA_LOWRANK].set(decay_up_b)

    h, z = _inproj(x, norm1_g.reshape(1, D), w_main, wz, tm=1024, tn=1024)
    ma = _conv_branch(h, conv_w, conv_b.reshape(1, D), wa_proj.astype(BF16),
                      gate_bias[0:1], seq, tm=512, col=col)
    of, ob = _gla(h, z, upf, decay_bias_f.reshape(1, -1), upb, decay_bias_b.reshape(1, -1),
                  batch, seq, tb=512, col=col, dk_total=dk_total, dv_total=dv_total)
    nh, _, nk, qd = peer_keys.shape
    keys = peer_keys.reshape(nh * 2, nk, qd).astype(BF16)
    x2, xn2, st = _post(x, of, ob, h, ma, gla_norm_g.reshape(1, -1), wb_proj.astype(BF16),
                        gate_bias[1:2], w_out.astype(BF16), norm2_g.reshape(1, D),
                        peer_wq.astype(BF16), keys, tm=256, col=col)
    r2, e2, n1, e1 = _topk(st, tn=256)
    return _peer(xn2, x2, peer_u.astype(BF16), peer_v.astype(BF16).T, r2, e2, n1, e1,
                 out_g.reshape(1, D), tn=512, eb=1024)


def kernel(x, norm1_g, w_in, conv_w, conv_b, wa_proj, decay_up_f, decay_bias_f, decay_up_b,
           decay_bias_b, gla_norm_g, wb_proj, gate_bias, w_out, norm2_g, peer_wq, peer_keys,
           peer_u, peer_v, final_norm_g):
    batch, seq, D = x.shape
    depth = w_in.shape[0]
    assert depth == 1, "the final norm is fused into the last layer's PEER kernel"
    y = _layer(x.reshape(batch * seq, D), norm1_g[0], w_in[0], conv_w[0], conv_b[0], wa_proj[0],
               decay_up_f[0], decay_bias_f[0], decay_up_b[0], decay_bias_b[0], gla_norm_g[0],
               wb_proj[0], gate_bias[0], w_out[0], norm2_g[0], peer_wq[0], peer_keys[0],
               peer_u[0], peer_v[0], final_norm_g, batch, seq)
    return y.reshape(batch, seq, D)
```

```python
import functools

import jax
import jax.numpy as jnp
import numpy as np
from jax import lax
from jax.experimental import pallas as pl
from jax.experimental.pallas import tpu as pltpu

F32 = jnp.float32
BF16 = jnp.bfloat16

EPS = 1e-6
CONV_WIDTH = 3
GLA_HEADS = 4
GLA_LOWRANK = 16
GLA_TAU = 16.0
GLA_CHUNK = 64
PEER_HEADS = 8
PEER_NKEYS = 128
PEER_TOPK = 16

VMEM_LIMIT_BYTES = 56 * 1024 * 1024
LANES = 128
SUBLANES = 8

NEG_INF = float("-inf")


def _cparams(sem):
    return pltpu.CompilerParams(dimension_semantics=sem, vmem_limit_bytes=VMEM_LIMIT_BYTES)


def _rms(x, g):
    return x * lax.rsqrt(jnp.mean(x * x, axis=-1, keepdims=True) + EPS) * g


def _inproj_kernel(x_ref, g_ref, w_ref, wz_ref, h_ref, z_ref, xn_ref):
    @pl.when(pl.program_id(1) == 0)
    def _():
        xn_ref[...] = _rms(x_ref[...], g_ref[...]).astype(BF16)
        z_ref[...] = jnp.dot(xn_ref[...], wz_ref[...], preferred_element_type=F32)

    h_ref[...] = jnp.dot(xn_ref[...], w_ref[...], preferred_element_type=F32).astype(BF16)


def _inproj(x, g, w, wz, tm, tn):
    T, D = x.shape
    N = w.shape[1]
    return pl.pallas_call(
        _inproj_kernel,
        grid=(T // tm, N // tn),
        in_specs=[
            pl.BlockSpec((tm, D), lambda i, j: (i, 0)),
            pl.BlockSpec((1, D), lambda i, j: (0, 0)),
            pl.BlockSpec((D, tn), lambda i, j: (0, j)),
            pl.BlockSpec((D, LANES), lambda i, j: (0, 0)),
        ],
        out_specs=[
            pl.BlockSpec((tm, tn), lambda i, j: (i, j)),
            pl.BlockSpec((tm, LANES), lambda i, j: (i, 0)),
        ],
        out_shape=[
            jax.ShapeDtypeStruct((T, N), BF16),
            jax.ShapeDtypeStruct((T, LANES), F32),
        ],
        scratch_shapes=[pltpu.VMEM((tm, D), BF16)],
        compiler_params=_cparams(("parallel", "arbitrary")),
        name="inproj",
    )(x, g, w, wz)


def _conv_kernel(tiles_per_seq, xa_ref, ba_ref, ca_ref, ga_ref, xap_ref, cap_ref, xan_ref, can_ref,
                 cw_ref, cb_ref, wa_ref, gbias_ref, o_ref):
    i = pl.program_id(0)
    tm = xa_ref.shape[0]
    u = ca_ref[...].astype(F32) * xa_ref[...].astype(F32)
    has_prev = (i % tiles_per_seq != 0).astype(F32)
    has_next = (i % tiles_per_seq != tiles_per_seq - 1).astype(F32)
    up_row = (cap_ref[...].astype(F32) * xap_ref[...].astype(F32))[SUBLANES - 1:SUBLANES, :] * has_prev
    un_row = (can_ref[...].astype(F32) * xan_ref[...].astype(F32))[0:1, :] * has_next
    row = lax.broadcasted_iota(jnp.int32, u.shape, 0)
    u_prev = jnp.where(row == 0, up_row, pltpu.roll(u, 1, axis=0))
    u_next = jnp.where(row == tm - 1, un_row, pltpu.roll(u, tm - 1, axis=0))
    cw = cw_ref[...]
    y = cb_ref[...] + u_prev * cw[0:1, :] + u * cw[1:2, :] + u_next * cw[2:3, :]
    ya = (ba_ref[...].astype(F32) * y).astype(BF16)
    out_a = jnp.dot(ya, wa_ref[...], preferred_element_type=F32)
    gate = jax.nn.sigmoid(ga_ref[...].astype(F32) + gbias_ref[...])
    o_ref[...] = gate * out_a


def _conv_branch(h, conv_w, conv_b, wa, gbias_a, seq, tm, col):
    T = h.shape[0]
    D = conv_w.shape[1]
    nb = D // D
    del nb
    tiles_per_seq = seq // tm
    r8 = tm // SUBLANES
    last8 = T // SUBLANES - 1

    def cblk(c):
        return pl.BlockSpec((tm, D), lambda i, c=c: (i, c))

    def prev_blk(c):
        return pl.BlockSpec((SUBLANES, D), lambda i, c=c: (jnp.maximum(i * r8 - 1, 0), c))

    def next_blk(c):
        return pl.BlockSpec((SUBLANES, D), lambda i, c=c: (jnp.minimum((i + 1) * r8, last8), c))

    def full(shape):
        return pl.BlockSpec(shape, lambda i: (0,) * len(shape))

    return pl.pallas_call(
        functools.partial(_conv_kernel, tiles_per_seq),
        grid=(T // tm,),
        in_specs=[
            cblk(col["xa"]), cblk(col["ba"]), cblk(col["ca"]), cblk(col["ga"]),
            prev_blk(col["xa"]), prev_blk(col["ca"]), next_blk(col["xa"]), next_blk(col["ca"]),
            full((CONV_WIDTH, D)), full((1, D)), full((D, D)), full((1, D)),
        ],
        out_specs=pl.BlockSpec((tm, D), lambda i: (i, 0)),
        out_shape=jax.ShapeDtypeStruct((T, D), F32),
        compiler_params=_cparams(("parallel",)),
        name="conv_branch",
    )(h, h, h, h, h, h, h, h, conv_w, conv_b, wa, gbias_a)


def _log_sigmoid(x):
    return jnp.minimum(x, 0.0) - jnp.log1p(jnp.exp(-jnp.abs(x)))


def _gla_chunk(q, k, v, la, st, tri, reverse):
    c = q.shape[0]
    b = jnp.dot(tri, la, preferred_element_type=F32, precision=lax.Precision.HIGHEST)
    if reverse:
        b_end = b[0:1, :]
        b_mid = b[c // 2 - 1:c // 2, :]
    else:
        b_end = b[c - 1:c, :]
        b_mid = b[c // 2:c // 2 + 1, :]
    q_in = (q * jnp.exp(b - b_mid)).astype(BF16)
    k_in = (k * jnp.exp(b_mid - b)).astype(BF16)
    scores = lax.dot_general(q_in, k_in, (((1,), (1,)), ((), ())), preferred_element_type=F32)
    ri = lax.broadcasted_iota(jnp.int32, scores.shape, 0)
    ci = lax.broadcasted_iota(jnp.int32, scores.shape, 1)
    keep = (ci >= ri) if reverse else (ci <= ri)
    scores = jnp.where(keep, scores, 0.0).astype(BF16)
    vb = v.astype(BF16)
    o = jnp.dot(scores, vb, preferred_element_type=F32)
    q_out = (q * jnp.exp(b)).astype(BF16)
    o = o + lax.dot_general(q_out, st.astype(BF16), (((1,), (1,)), ((), ())),
                            preferred_element_type=F32)
    k_end = (k * jnp.exp(b_end - b)).astype(BF16)
    ut = lax.dot_general(vb, k_end, (((0,), (0,)), ((), ())), preferred_element_type=F32)
    st_new = st * jnp.exp(b_end) + ut
    return o, st_new


def _gla_kernel(qf_ref, kf_ref, vf_ref, zf_ref, qb_ref, kb_ref, vb_ref, zb_ref,
                upf_ref, bf_ref, upb_ref, bb_ref, of_ref, ob_ref,
                sf_ref, sb_ref, laf_ref, lab_ref):
    tb = qf_ref.shape[0]
    dk = qf_ref.shape[1] // GLA_HEADS
    dv = vf_ref.shape[1] // GLA_HEADS
    nchunk = tb // GLA_CHUNK
    scale = dk ** -0.5

    @pl.when(pl.program_id(1) == 0)
    def _():
        sf_ref[...] = jnp.zeros_like(sf_ref)
        sb_ref[...] = jnp.zeros_like(sb_ref)

    hi = lax.Precision.HIGHEST
    laf_ref[...] = _log_sigmoid(
        jnp.dot(zf_ref[...], upf_ref[...], preferred_element_type=F32, precision=hi) + bf_ref[...]
    ) * (1.0 / GLA_TAU)
    lab_ref[...] = _log_sigmoid(
        jnp.dot(zb_ref[...], upb_ref[...], preferred_element_type=F32, precision=hi) + bb_ref[...]
    ) * (1.0 / GLA_TAU)

    ri = lax.broadcasted_iota(jnp.int32, (GLA_CHUNK, GLA_CHUNK), 0)
    ci = lax.broadcasted_iota(jnp.int32, (GLA_CHUNK, GLA_CHUNK), 1)
    tril = (ci <= ri).astype(F32)
    triu = (ci >= ri).astype(F32)

    def body(c, carry):
        rf = pl.ds(pl.multiple_of(c * GLA_CHUNK, GLA_CHUNK), GLA_CHUNK)
        rb = pl.ds(pl.multiple_of((nchunk - 1 - c) * GLA_CHUNK, GLA_CHUNK), GLA_CHUNK)
        for h in range(GLA_HEADS):
            ck = slice(h * dk, (h + 1) * dk)
            cv = slice(h * dv, (h + 1) * dv)
            o, st = _gla_chunk(qf_ref[rf, ck].astype(F32) * scale, kf_ref[rf, ck].astype(F32),
                               vf_ref[rf, cv].astype(F32), laf_ref[rf, ck], sf_ref[h], tril, False)
            of_ref[rf, cv] = o
            sf_ref[h] = st
            o, st = _gla_chunk(qb_ref[rb, ck].astype(F32) * scale, kb_ref[rb, ck].astype(F32),
                               vb_ref[rb, cv].astype(F32), lab_ref[rb, ck], sb_ref[h], triu, True)
            ob_ref[rb, cv] = o
            sb_ref[h] = st
        return carry

    lax.fori_loop(0, nchunk, body, 0)


def _gla(h, z, upf, bf, upb, bb, batch, seq, tb, col, dk_total, dv_total):
    T = h.shape[0]
    nblk = seq // tb

    def fwd(w, c):
        return pl.BlockSpec((tb, w), lambda b, i, c=c: (b * nblk + i, c))

    def bwd(w, c):
        return pl.BlockSpec((tb, w), lambda b, i, c=c: (b * nblk + nblk - 1 - i, c))

    def full(shape):
        return pl.BlockSpec(shape, lambda b, i: (0,) * len(shape))

    dk = dk_total // GLA_HEADS
    dv = dv_total // GLA_HEADS
    return pl.pallas_call(
        _gla_kernel,
        grid=(batch, nblk),
        in_specs=[
            fwd(dk_total, col["q"]), fwd(dk_total, col["k"]), fwd(dv_total, col["v"]), fwd(LANES, 0),
            bwd(dk_total, col["q"]), bwd(dk_total, col["k"]), bwd(dv_total, col["v"]), bwd(LANES, 0),
            full((LANES, dk_total)), full((1, dk_total)), full((LANES, dk_total)), full((1, dk_total)),
        ],
        out_specs=[fwd(dv_total, 0), bwd(dv_total, 0)],
        out_shape=[jax.ShapeDtypeStruct((T, dv_total), F32)] * 2,
        scratch_shapes=[
            pltpu.VMEM((GLA_HEADS, dv, dk), F32),
            pltpu.VMEM((GLA_HEADS, dv, dk), F32),
            pltpu.VMEM((tb, dk_total), F32),
            pltpu.VMEM((tb, dk_total), F32),
        ],
        compiler_params=_cparams(("parallel", "arbitrary")),
        name="gla",
    )(h, h, h, z, h, h, h, z, upf, bf, upb, bb)


def _post_kernel(x_ref, of_ref, ob_ref, r_ref, gb_ref, ma_ref, gn_ref, wb_ref, gbias_ref, wo_ref,
                 g2_ref, wq_ref, keys_ref, x2_ref, xn2_ref, st_ref):
    dv = of_ref.shape[1] // GLA_HEADS
    o = of_ref[...] + ob_ref[...]
    gn = gn_ref[...]
    parts = []
    for h in range(GLA_HEADS):
        cs = slice(h * dv, (h + 1) * dv)
        parts.append(_rms(o[:, cs], gn[:, cs]))
    o = jnp.concatenate(parts, axis=-1)
    r = r_ref[...].astype(F32)
    o = (o * (r * jax.nn.sigmoid(r))).astype(BF16)
    out_b = jnp.dot(o, wb_ref[...], preferred_element_type=F32)
    gate_b = jax.nn.sigmoid(gb_ref[...].astype(F32) + gbias_ref[...])
    m = (ma_ref[...] + gate_b * out_b).astype(BF16)
    x2 = x_ref[...] + jnp.dot(m, wo_ref[...], preferred_element_type=F32)
    x2_ref[...] = x2
    xn2 = _rms(x2, g2_ref[...]).astype(BF16)
    xn2_ref[...] = pltpu.bitcast(xn2, jnp.uint32)
    q = jnp.dot(xn2, wq_ref[...], preferred_element_type=F32).astype(BF16)
    qd = keys_ref.shape[2]
    for hp in range(keys_ref.shape[0]):
        st_ref[hp] = lax.dot_general(keys_ref[hp], q[:, hp * qd:(hp + 1) * qd],
                                     (((1,), (1,)), ((), ())), preferred_element_type=F32)


def _post(x, of, ob, h, ma, gn, wb, gbias_b, wo, g2, wq, keys, tm, col):
    T, D = x.shape
    nhp, nk, qd = keys.shape

    def tok(w, c=0):
        return pl.BlockSpec((tm, w), lambda i, c=c: (i, c))

    def full(shape):
        return pl.BlockSpec(shape, lambda i: (0,) * len(shape))

    return pl.pallas_call(
        _post_kernel,
        grid=(T // tm,),
        in_specs=[
            tok(D), tok(D), tok(D), tok(D, col["r"]), tok(D, col["gb"]), tok(D),
            full((1, D)), full((D, D)), full((1, D)), full((D, D)),
            full((1, D)), full(wq.shape), full(keys.shape),
        ],
        out_specs=[tok(D), pl.BlockSpec((tm // 2, D), lambda i: (i, 0)),
                   pl.BlockSpec((nhp, nk, tm), lambda i: (0, 0, i))],
        out_shape=[
            jax.ShapeDtypeStruct((T, D), F32),
            jax.ShapeDtypeStruct((T // 2, D), jnp.uint32),
            jax.ShapeDtypeStruct((nhp, nk, T), F32),
        ],
        compiler_params=_cparams(("parallel",)),
        name="post",
    )(x, of, ob, h, h, ma, gn, wb, gbias_b, wo, g2, wq, keys)


def _cand_rows():
    K = PEER_TOPK
    pieces = []
    for r1 in range(K // 2):
        lim = K // (r1 + 1)
        for start in range(0, lim, SUBLANES):
            pieces.append(("row", r1, start, min(lim - start, SUBLANES)))
    pieces.append(("col", K // 2, 0, SUBLANES))
    return pieces


def _top_ranks(a, k):
    n = a.shape[0]
    row = lax.broadcasted_iota(jnp.int32, a.shape, 0).astype(F32)
    rank = jnp.full(a.shape, float(k), F32)
    tops = []
    for r in range(k):
        m = jnp.max(a, axis=0, keepdims=True)
        first = jnp.min(jnp.where(a == m, row, float(n)), axis=0, keepdims=True)
        sel = row == first
        rank = jnp.where(sel, float(r), rank)
        a = jnp.where(sel, NEG_INF, a)
        tops.append(m)
    return rank, tops


def _pack_bf16(x):
    return pltpu.bitcast(x.astype(BF16), jnp.uint32)


def _unpack_bf16(words):
    return pltpu.bitcast(words, BF16)


def _stack_rows(rows, sub):
    out = jnp.broadcast_to(rows[0], sub.shape)
    for d in range(1, len(rows)):
        out = jnp.where(sub == d, rows[d], out)
    return out


def _topk_kernel(s_ref, r2_ref, e2_ref, n1_ref, e1_ref):
    K = PEER_TOPK
    L = s_ref.shape[2]
    pieces = _cand_rows()
    sub = lax.broadcasted_iota(jnp.int32, (SUBLANES, L), 0)
    for h in range(PEER_HEADS):
        s1 = s_ref[2 * h]
        s2 = s_ref[2 * h + 1]
        rank1, top1 = _top_ranks(s1, K)
        rank2, top2 = _top_ranks(s2, K)
        t1_hi = _stack_rows(top1[K // 2:], sub)
        t2_lo = _stack_rows(top2[:SUBLANES], sub)
        t2_hi = _stack_rows(top2[SUBLANES:], sub)
        cands, flats = [], []
        for kind, r1, start, cnt in pieces:
            if kind == "row":
                c = top1[r1] + (t2_lo if start == 0 else t2_hi)
                c = jnp.where(sub < cnt, c, NEG_INF)
                f = (r1 * K + start + sub).astype(F32)
            else:
                c = t1_hi + top2[0]
                f = ((r1 + sub) * K).astype(F32)
            cands.append(c)
            flats.append(f)
        cand = jnp.concatenate(cands, axis=0)
        flat = jnp.concatenate(flats, axis=0)
        cmax = top1[0] + top2[0]
        work = cand
        taken = jnp.zeros(cand.shape, F32)
        for _ in range(K):
            m = jnp.max(work, axis=0, keepdims=True)
            first = jnp.min(jnp.where(work == m, flat, float(K * K)), axis=0, keepdims=True)
            sel = flat == first
            taken = jnp.where(sel, 1.0, taken)
            work = jnp.where(sel, NEG_INF, work)
        z = jnp.sum(jnp.where(taken > 0.0, jnp.exp(cand - cmax), 0.0), axis=0, keepdims=True)
        n1 = jnp.zeros(s1.shape, F32)
        off = 0
        counts = {}
        for kind, r1, start, cnt in pieces:
            blk = taken[off:off + SUBLANES]
            off += SUBLANES
            if kind == "row":
                counts[r1] = counts.get(r1, 0.0) + jnp.sum(blk, axis=0, keepdims=True)
            else:
                for d in range(SUBLANES):
                    counts[r1 + d] = blk[d:d + 1]
        for r1 in range(K):
            n1 = jnp.where(rank1 == float(r1), counts[r1], n1)
        r2p = _pack_bf16(rank2)
        e2p = _pack_bf16(jnp.exp(s2 - top2[0]) / z)
        for c in range(L // LANES):
            r2_ref[h, c] = r2p[:, c * LANES:(c + 1) * LANES]
            e2_ref[h, c] = e2p[:, c * LANES:(c + 1) * LANES]
        n1_ref[h] = n1
        e1_ref[h] = jnp.exp(s1 - top1[0])


def _topk(st, tn):
    nhp, nk, T = st.shape

    def spec(rows):
        return pl.BlockSpec((PEER_HEADS, rows, tn), lambda i: (0, 0, i))

    slab_spec = pl.BlockSpec((PEER_HEADS, tn // LANES, nk // 2, LANES), lambda i: (0, i, 0, 0))
    slab = jax.ShapeDtypeStruct((PEER_HEADS, T // LANES, nk // 2, LANES), jnp.uint32)
    rows = jax.ShapeDtypeStruct((PEER_HEADS, nk, T), F32)
    return pl.pallas_call(
        _topk_kernel,
        grid=(T // tn,),
        in_specs=[pl.BlockSpec((nhp, nk, tn), lambda i: (0, 0, i))],
        out_specs=[slab_spec, slab_spec, spec(nk), spec(nk)],
        out_shape=[slab, slab, rows, rows],
        compiler_params=_cparams(("parallel",)),
        name="topk",
    )(st)


def _gelu(x):
    return 0.5 * x * (1.0 + lax.erf(x * (2.0 ** -0.5)))


def _peer_kernel(nj, xncur_ref, xnnext_ref, x2_ref, ucur_ref, unext_ref, vtprev_ref, vtcur_ref,
                 r2_ref, e2_ref, n1_ref, e1_ref, gf_ref, o_ref,
                 acc_ref, hid_a, hid_b, w_a, w_b):
    s = pl.program_id(0)
    j = s % nj
    nk = 2 * r2_ref.shape[2]
    half = ucur_ref.shape[0]
    per_half = half // nk

    tn = 2 * xncur_ref.shape[0]
    mm_cols = tn // 2

    def hidden(xn_ref, u_ref, urows, hid_ref, c):
        cols = slice(c * mm_cols, (c + 1) * mm_cols)
        xrows = slice(c * mm_cols // 2, (c + 1) * mm_cols // 2)
        hid_ref[:, cols] = lax.dot_general(_unpack_bf16(u_ref[urows, :]), _unpack_bf16(xn_ref[xrows, :]),
                                           (((1,), (1,)), ((), ())), preferred_element_type=F32)

    def accumulate(vt_ref, ecols, w_ref, c):
        cols = slice(c * mm_cols, (c + 1) * mm_cols)
        acc_ref[:, cols] += jnp.dot(_unpack_bf16(vt_ref[:, ecols]), _unpack_bf16(w_ref[:, cols]),
                                    preferred_element_type=F32)

    def weights(hid_ref, w_ref, i1, a):
        rows = slice(a * nk, (a + 1) * nk)
        wrows = slice(a * nk // 2, (a + 1) * nk // 2)
        pk = 2 * SUBLANES

        def row_tile(ref, h, cols):
            return jnp.broadcast_to(ref[h, i1:i1 + 1, cols], (pk, LANES)).astype(BF16)[None]

        for c in range(tn // LANES):
            cols = slice(c * LANES, (c + 1) * LANES)
            g = None
            for h in range(PEER_HEADS):
                n1 = row_tile(n1_ref, h, cols)
                e1 = row_tile(e1_ref, h, cols)
                r2 = _unpack_bf16(r2_ref[h, c]).reshape(nk // pk, pk, LANES)
                e2 = _unpack_bf16(e2_ref[h, c]).reshape(nk // pk, pk, LANES)
                t = jnp.where(r2 < n1, e2, jnp.zeros((), BF16)) * e1
                g = t if g is None else g + t
            act = _gelu(hid_ref[rows, cols]).astype(BF16)
            w_ref[wrows, cols] = pltpu.bitcast(act * g.reshape(nk, LANES), jnp.uint32)

    def half_step(hid_cur, w_cur, first_i1, xn_ref, u_ref, urows, hid_next, vt_ref, ecols, w_prev):
        assert per_half == 4
        hidden(xn_ref, u_ref, urows, hid_next, 0)
        weights(hid_cur, w_cur, first_i1, 0)
        accumulate(vt_ref, ecols, w_prev, 0)
        weights(hid_cur, w_cur, first_i1 + 1, 1)
        hidden(xn_ref, u_ref, urows, hid_next, 1)
        weights(hid_cur, w_cur, first_i1 + 2, 2)
        accumulate(vt_ref, ecols, w_prev, 1)
        weights(hid_cur, w_cur, first_i1 + 3, 3)

    lo, hi = slice(0, half), slice(half, 2 * half)
    ulo, uhi = slice(0, half // 2), slice(half // 2, half)

    @pl.when(s == 0)
    def _():
        hidden(xncur_ref, ucur_ref, ulo, hid_a, 0)
        hidden(xncur_ref, ucur_ref, ulo, hid_a, 1)
        w_b[...] = jnp.zeros_like(w_b)
        acc_ref[...] = jnp.zeros_like(acc_ref)

    half_step(hid_a, w_a, 0, xncur_ref, ucur_ref, uhi, hid_b, vtprev_ref, hi, w_b)

    @pl.when(jnp.logical_and(j == 0, s > 0))
    def _():
        x3 = x2_ref[...] + acc_ref[...].T
        o_ref[...] = _rms(x3, gf_ref[...])

    @pl.when(j == 0)
    def _():
        acc_ref[...] = jnp.zeros_like(acc_ref)

    half_step(hid_b, w_b, per_half, xnnext_ref, unext_ref, ulo, hid_a, vtcur_ref, lo, w_a)


def _pack_tables_kernel(u_ref, v_ref, up_ref, vtp_ref):
    up_ref[...] = _pack_bf16(u_ref[...])
    vtp_ref[...] = _pack_bf16(v_ref[...].T)


def _pack_tables(u, v, eb):
    E, D = u.shape
    return pl.pallas_call(
        _pack_tables_kernel,
        grid=(E // eb,),
        in_specs=[pl.BlockSpec((eb, D), lambda j: (j, 0)), pl.BlockSpec((eb, D), lambda j: (j, 0))],
        out_specs=[pl.BlockSpec((eb // 2, D), lambda j: (j, 0)),
                   pl.BlockSpec((D // 2, eb), lambda j: (0, j))],
        out_shape=[jax.ShapeDtypeStruct((E // 2, D), jnp.uint32),
                   jax.ShapeDtypeStruct((D // 2, E), jnp.uint32)],
        compiler_params=_cparams(("parallel",)),
        name="pack_tables",
    )(u, v)


def _peer(xn2, x2, u, vt, r2, e2, n1, e1, gf, tn, eb):
    T, D = x2.shape
    E = 2 * u.shape[0]
    ni, nj = T // tn, E // eb
    nsteps = ni * nj + 1

    def tile_of(s):
        return jnp.minimum(s // nj, ni - 1)

    def tab(t):
        return pl.BlockSpec((PEER_HEADS, tn // LANES) + t.shape[2:], lambda s: (0, tile_of(s), 0, 0))

    nk = n1.shape[1]
    per_blk = eb // nk
    n1 = n1.reshape(PEER_HEADS, nj, per_blk, T)
    e1 = e1.reshape(PEER_HEADS, nj, per_blk, T)
    row_tab = pl.BlockSpec((PEER_HEADS, None, per_blk, tn), lambda s: (0, s % nj, 0, tile_of(s)))
    done_tile = pl.BlockSpec((tn, D), lambda s: (jnp.maximum(s - 1, 0) // nj, 0))
    return pl.pallas_call(
        functools.partial(_peer_kernel, nj),
        grid=(nsteps,),
        in_specs=[
            pl.BlockSpec((tn // 2, D), lambda s: (tile_of(s), 0)),
            pl.BlockSpec((tn // 2, D), lambda s: (tile_of(s + 1), 0)),
            done_tile,
            pl.BlockSpec((eb // 2, D), lambda s: (s % nj, 0)),
            pl.BlockSpec((eb // 2, D), lambda s: ((s + 1) % nj, 0)),
            pl.BlockSpec((D // 2, eb), lambda s: (0, (s + nj - 1) % nj)),
            pl.BlockSpec((D // 2, eb), lambda s: (0, s % nj)),
            tab(r2), tab(e2), row_tab, row_tab,
            pl.BlockSpec((1, D), lambda s: (0, 0)),
        ],
        out_specs=done_tile,
        out_shape=jax.ShapeDtypeStruct((T, D), F32),
        scratch_shapes=[
            pltpu.VMEM((D, tn), F32),
            pltpu.VMEM((eb // 2, tn), F32),
            pltpu.VMEM((eb // 2, tn), F32),
            pltpu.VMEM((eb // 4, tn), jnp.uint32),
            pltpu.VMEM((eb // 4, tn), jnp.uint32),
        ],
        compiler_params=_cparams(("arbitrary",)),
        name="peer",
    )(xn2, xn2, x2, u, u, vt, vt, r2, e2, n1, e1, gf)


def _layer(x, norm1_g, w_in, conv_w, conv_b, wa_proj, decay_up_f, decay_bias_f, decay_up_b,
           decay_bias_b, gla_norm_g, wb_proj, gate_bias, w_out, norm2_g, peer_wq, peer_keys,
           peer_u, peer_v, out_g, batch, seq):
    T, D = x.shape
    dk_total = decay_up_f.shape[1]
    dv_total = wb_proj.shape[0]
    d_conv = conv_w.shape[1]
    assert d_conv == D and dv_total == D and 2 * dk_total == D

    sizes = (d_conv, d_conv, d_conv, dk_total, dk_total, dv_total, dv_total,
             GLA_LOWRANK, GLA_LOWRANK, D, D)
    offs = np.concatenate([[0], np.cumsum(sizes)])
    names = ("xa", "ba", "ca", "q", "k", "v", "r", "zf", "zb", "ga", "gb")
    seg = {n: (int(offs[i]), int(offs[i + 1])) for i, n in enumerate(names)}
    main = ("xa", "ba", "ca", "q", "k", "v", "r", "ga", "gb")
    w_main = jnp.concatenate([w_in[:, seg[n][0]:seg[n][1]] for n in main], axis=1).astype(BF16)
    wz = jnp.concatenate([w_in[:, seg["zf"][0]:seg["zb"][1]],
                          jnp.zeros((D, LANES - 2 * GLA_LOWRANK), w_in.dtype)], axis=1).astype(BF16)
    col, o = {}, 0
    for n in main:
        wdt = seg[n][1] - seg[n][0]
        col[n] = o // wdt
        o += wdt

    upf = jnp.zeros((LANES, dk_total), F32).at[:GLA_LOWRANK].set(decay_up_f)
    upb = jnp.zeros((LANES, dk_total), F32).at[GLA_LOWRANK:2 * GLA_LOWRANK].set(decay_up_b)

    h, z = _inproj(x, norm1_g.reshape(1, D), w_main, wz, tm=1024, tn=1024)
    ma = _conv_branch(h, conv_w, conv_b.reshape(1, D), wa_proj.astype(BF16),
                      gate_bias[0:1], seq, tm=512, col=col)
    of, ob = _gla(h, z, upf, decay_bias_f.reshape(1, -1), upb, decay_bias_b.reshape(1, -1),
                  batch, seq, tb=512, col=col, dk_total=dk_total, dv_total=dv_total)
    nh, _, nk, qd = peer_keys.shape
    keys = peer_keys.reshape(nh * 2, nk, qd).astype(BF16)
    x2, xn2, st = _post(x, of, ob, h, ma, gla_norm_g.reshape(1, -1), wb_proj.astype(BF16),
                        gate_bias[1:2], w_out.astype(BF16), norm2_g.reshape(1, D),
                        peer_wq.astype(BF16), keys, tm=256, col=col)
    r2, e2, n1, e1 = _topk(st, tn=256)
    up, vtp = _pack_tables(peer_u, peer_v, eb=512)
    return _peer(xn2, x2, up, vtp, r2, e2, n1, e1, out_g.reshape(1, D), tn=512, eb=1024)


def kernel(x, norm1_g, w_in, conv_w, conv_b, wa_proj, decay_up_f, decay_bias_f, decay_up_b,
           decay_bias_b, gla_norm_g, wb_proj, gate_bias, w_out, norm2_g, peer_wq, peer_keys,
           peer_u, peer_v, final_norm_g):
    batch, seq, D = x.shape
    depth = w_in.shape[0]
    assert depth == 1, "the final norm is fused into the last layer's PEER kernel"
    y = _layer(x.reshape(batch * seq, D), norm1_g[0], w_in[0], conv_w[0], conv_b[0], wa_proj[0],
               decay_up_f[0], decay_bias_f[0], decay_up_b[0], decay_bias_b[0], gla_norm_g[0],
               wb_proj[0], gate_bias[0], w_out[0], norm2_g[0], peer_wq[0], peer_keys[0],
               peer_u[0], peer_v[0], final_norm_g, batch, seq)
    return y.reshape(batch, seq, D)
```

```python
import functools

import jax
import jax.numpy as jnp
import numpy as np
from jax import lax
from jax.experimental import pallas as pl
from jax.experimental.pallas import tpu as pltpu

F32 = jnp.float32
BF16 = jnp.bfloat16

EPS = 1e-6
CONV_WIDTH = 3
GLA_HEADS = 4
GLA_LOWRANK = 16
GLA_TAU = 16.0
GLA_CHUNK = 64
PEER_HEADS = 8
PEER_NKEYS = 128
PEER_TOPK = 16

VMEM_LIMIT_BYTES = 56 * 1024 * 1024
LANES = 128
SUBLANES = 8

NEG_INF = float("-inf")


def _cparams(sem):
    return pltpu.CompilerParams(dimension_semantics=sem, vmem_limit_bytes=VMEM_LIMIT_BYTES)


def _rms(x, g):
    return x * lax.rsqrt(jnp.mean(x * x, axis=-1, keepdims=True) + EPS) * g


def _inproj_kernel(x_ref, g_ref, w_ref, wz_ref, h_ref, z_ref, xn_ref):
    @pl.when(pl.program_id(1) == 0)
    def _():
        xn_ref[...] = _rms(x_ref[...], g_ref[...]).astype(BF16)
        z_ref[...] = jnp.dot(xn_ref[...], wz_ref[...], preferred_element_type=F32)

    h_ref[...] = jnp.dot(xn_ref[...], w_ref[...], preferred_element_type=F32).astype(BF16)


def _inproj(x, g, w, wz, tm, tn):
    T, D = x.shape
    N = w.shape[1]
    return pl.pallas_call(
        _inproj_kernel,
        grid=(T // tm, N // tn),
        in_specs=[
            pl.BlockSpec((tm, D), lambda i, j: (i, 0)),
            pl.BlockSpec((1, D), lambda i, j: (0, 0)),
            pl.BlockSpec((D, tn), lambda i, j: (0, j)),
            pl.BlockSpec((D, LANES), lambda i, j: (0, 0)),
        ],
        out_specs=[
            pl.BlockSpec((tm, tn), lambda i, j: (i, j)),
            pl.BlockSpec((tm, LANES), lambda i, j: (i, 0)),
        ],
        out_shape=[
            jax.ShapeDtypeStruct((T, N), BF16),
            jax.ShapeDtypeStruct((T, LANES), F32),
        ],
        scratch_shapes=[pltpu.VMEM((tm, D), BF16)],
        compiler_params=_cparams(("parallel", "arbitrary")),
        name="inproj",
    )(x, g, w, wz)


def _conv_kernel(tiles_per_seq, xa_ref, ba_ref, ca_ref, ga_ref, xap_ref, cap_ref, xan_ref, can_ref,
                 cw_ref, cb_ref, wa_ref, gbias_ref, o_ref):
    i = pl.program_id(0)
    tm = xa_ref.shape[0]
    u = ca_ref[...].astype(F32) * xa_ref[...].astype(F32)
    has_prev = (i % tiles_per_seq != 0).astype(F32)
    has_next = (i % tiles_per_seq != tiles_per_seq - 1).astype(F32)
    up_row = (cap_ref[...].astype(F32) * xap_ref[...].astype(F32))[SUBLANES - 1:SUBLANES, :] * has_prev
    un_row = (can_ref[...].astype(F32) * xan_ref[...].astype(F32))[0:1, :] * has_next
    row = lax.broadcasted_iota(jnp.int32, u.shape, 0)
    u_prev = jnp.where(row == 0, up_row, pltpu.roll(u, 1, axis=0))
    u_next = jnp.where(row == tm - 1, un_row, pltpu.roll(u, tm - 1, axis=0))
    cw = cw_ref[...]
    y = cb_ref[...] + u_prev * cw[0:1, :] + u * cw[1:2, :] + u_next * cw[2:3, :]
    ya = (ba_ref[...].astype(F32) * y).astype(BF16)
    out_a = jnp.dot(ya, wa_ref[...], preferred_element_type=F32)
    gate = jax.nn.sigmoid(ga_ref[...].astype(F32) + gbias_ref[...])
    o_ref[...] = gate * out_a


def _conv_branch(h, conv_w, conv_b, wa, gbias_a, seq, tm, col):
    T = h.shape[0]
    D = conv_w.shape[1]
    nb = D // D
    del nb
    tiles_per_seq = seq // tm
    r8 = tm // SUBLANES
    last8 = T // SUBLANES - 1

    def cblk(c):
        return pl.BlockSpec((tm, D), lambda i, c=c: (i, c))

    def prev_blk(c):
        return pl.BlockSpec((SUBLANES, D), lambda i, c=c: (jnp.maximum(i * r8 - 1, 0), c))

    def next_blk(c):
        return pl.BlockSpec((SUBLANES, D), lambda i, c=c: (jnp.minimum((i + 1) * r8, last8), c))

    def full(shape):
        return pl.BlockSpec(shape, lambda i: (0,) * len(shape))

    return pl.pallas_call(
        functools.partial(_conv_kernel, tiles_per_seq),
        grid=(T // tm,),
        in_specs=[
            cblk(col["xa"]), cblk(col["ba"]), cblk(col["ca"]), cblk(col["ga"]),
            prev_blk(col["xa"]), prev_blk(col["ca"]), next_blk(col["xa"]), next_blk(col["ca"]),
            full((CONV_WIDTH, D)), full((1, D)), full((D, D)), full((1, D)),
        ],
        out_specs=pl.BlockSpec((tm, D), lambda i: (i, 0)),
        out_shape=jax.ShapeDtypeStruct((T, D), F32),
        compiler_params=_cparams(("parallel",)),
        name="conv_branch",
    )(h, h, h, h, h, h, h, h, conv_w, conv_b, wa, gbias_a)


def _log_sigmoid(x):
    return jnp.minimum(x, 0.0) - jnp.log1p(jnp.exp(-jnp.abs(x)))


def _chunk_cumsum(x, chunk, reverse):
    n = x.shape[0]
    pos = lax.broadcasted_iota(jnp.int32, x.shape, 0) & (chunk - 1)
    step = 1
    while step < chunk:
        if reverse:
            shifted = pltpu.roll(x, n - step, axis=0)
            valid = pos < chunk - step
        else:
            shifted = pltpu.roll(x, step, axis=0)
            valid = pos >= step
        x = x + jnp.where(valid, shifted, 0.0)
        step *= 2
    return x


def _gla_chunk(q, k, v, b, st, reverse):
    c = q.shape[0]
    if reverse:
        b_end = b[0:1, :]
        b_mid = b[c // 2 - 1:c // 2, :]
    else:
        b_end = b[c - 1:c, :]
        b_mid = b[c // 2:c // 2 + 1, :]
    q_in = (q * jnp.exp(b - b_mid)).astype(BF16)
    k_in = (k * jnp.exp(b_mid - b)).astype(BF16)
    scores = lax.dot_general(q_in, k_in, (((1,), (1,)), ((), ())), preferred_element_type=F32)
    ri = lax.broadcasted_iota(jnp.int32, scores.shape, 0)
    ci = lax.broadcasted_iota(jnp.int32, scores.shape, 1)
    keep = (ci >= ri) if reverse else (ci <= ri)
    scores = jnp.where(keep, scores, 0.0).astype(BF16)
    vb = v.astype(BF16)
    o = jnp.dot(scores, vb, preferred_element_type=F32)
    q_out = (q * jnp.exp(b)).astype(BF16)
    o = o + lax.dot_general(q_out, st.astype(BF16), (((1,), (1,)), ((), ())),
                            preferred_element_type=F32)
    k_end = (k * jnp.exp(b_end - b)).astype(BF16)
    ut = lax.dot_general(vb, k_end, (((0,), (0,)), ((), ())), preferred_element_type=F32)
    st_new = st * jnp.exp(b_end) + ut
    return o, st_new


def _gla_kernel(qf_ref, kf_ref, vf_ref, zf_ref, qb_ref, kb_ref, vb_ref, zb_ref,
                upf_ref, bf_ref, upb_ref, bb_ref, of_ref, ob_ref,
                sf_ref, sb_ref, bf_sum, bb_sum):
    tb = qf_ref.shape[0]
    dk = qf_ref.shape[1] // GLA_HEADS
    dv = vf_ref.shape[1] // GLA_HEADS
    nchunk = tb // GLA_CHUNK
    scale = dk ** -0.5

    @pl.when(pl.program_id(1) == 0)
    def _():
        sf_ref[...] = jnp.zeros_like(sf_ref)
        sb_ref[...] = jnp.zeros_like(sb_ref)

    hi = lax.Precision.HIGHEST

    def log_decay(z_ref, up_ref, bias_ref):
        pre = jnp.dot(z_ref[...], up_ref[...], preferred_element_type=F32, precision=hi) + bias_ref[...]
        return _log_sigmoid(pre) * (1.0 / GLA_TAU)

    bf_sum[...] = _chunk_cumsum(log_decay(zf_ref, upf_ref, bf_ref), GLA_CHUNK, False)
    bb_sum[...] = _chunk_cumsum(log_decay(zb_ref, upb_ref, bb_ref), GLA_CHUNK, True)

    def body(c, carry):
        rf = pl.ds(pl.multiple_of(c * GLA_CHUNK, GLA_CHUNK), GLA_CHUNK)
        rb = pl.ds(pl.multiple_of((nchunk - 1 - c) * GLA_CHUNK, GLA_CHUNK), GLA_CHUNK)
        for h in range(GLA_HEADS):
            ck = slice(h * dk, (h + 1) * dk)
            cv = slice(h * dv, (h + 1) * dv)
            o, st = _gla_chunk(qf_ref[rf, ck].astype(F32) * scale, kf_ref[rf, ck].astype(F32),
                               vf_ref[rf, cv].astype(F32), bf_sum[rf, ck], sf_ref[h], False)
            of_ref[rf, cv] = o
            sf_ref[h] = st
            o, st = _gla_chunk(qb_ref[rb, ck].astype(F32) * scale, kb_ref[rb, ck].astype(F32),
                               vb_ref[rb, cv].astype(F32), bb_sum[rb, ck], sb_ref[h], True)
            ob_ref[rb, cv] = o
            sb_ref[h] = st
        return carry

    lax.fori_loop(0, nchunk, body, 0)


def _gla(h, z, upf, bf, upb, bb, batch, seq, tb, col, dk_total, dv_total):
    T = h.shape[0]
    nblk = seq // tb

    def fwd(w, c):
        return pl.BlockSpec((tb, w), lambda b, i, c=c: (b * nblk + i, c))

    def bwd(w, c):
        return pl.BlockSpec((tb, w), lambda b, i, c=c: (b * nblk + nblk - 1 - i, c))

    def full(shape):
        return pl.BlockSpec(shape, lambda b, i: (0,) * len(shape))

    dk = dk_total // GLA_HEADS
    dv = dv_total // GLA_HEADS
    return pl.pallas_call(
        _gla_kernel,
        grid=(batch, nblk),
        in_specs=[
            fwd(dk_total, col["q"]), fwd(dk_total, col["k"]), fwd(dv_total, col["v"]), fwd(LANES, 0),
            bwd(dk_total, col["q"]), bwd(dk_total, col["k"]), bwd(dv_total, col["v"]), bwd(LANES, 0),
            full((LANES, dk_total)), full((1, dk_total)), full((LANES, dk_total)), full((1, dk_total)),
        ],
        out_specs=[fwd(dv_total, 0), bwd(dv_total, 0)],
        out_shape=[jax.ShapeDtypeStruct((T, dv_total), F32)] * 2,
        scratch_shapes=[
            pltpu.VMEM((GLA_HEADS, dv, dk), F32),
            pltpu.VMEM((GLA_HEADS, dv, dk), F32),
            pltpu.VMEM((tb, dk_total), F32),
            pltpu.VMEM((tb, dk_total), F32),
        ],
        compiler_params=_cparams(("parallel", "arbitrary")),
        name="gla",
    )(h, h, h, z, h, h, h, z, upf, bf, upb, bb)


def _post_kernel(x_ref, of_ref, ob_ref, r_ref, gb_ref, ma_ref, gn_ref, wb_ref, gbias_ref, wo_ref,
                 g2_ref, wq_ref, keys_ref, x2_ref, xn2_ref, st_ref):
    dv = of_ref.shape[1] // GLA_HEADS
    o = of_ref[...] + ob_ref[...]
    gn = gn_ref[...]
    parts = []
    for h in range(GLA_HEADS):
        cs = slice(h * dv, (h + 1) * dv)
        parts.append(_rms(o[:, cs], gn[:, cs]))
    o = jnp.concatenate(parts, axis=-1)
    r = r_ref[...].astype(F32)
    o = (o * (r * jax.nn.sigmoid(r))).astype(BF16)
    out_b = jnp.dot(o, wb_ref[...], preferred_element_type=F32)
    gate_b = jax.nn.sigmoid(gb_ref[...].astype(F32) + gbias_ref[...])
    m = (ma_ref[...] + gate_b * out_b).astype(BF16)
    x2 = x_ref[...] + jnp.dot(m, wo_ref[...], preferred_element_type=F32)
    x2_ref[...] = x2
    xn2 = _rms(x2, g2_ref[...]).astype(BF16)
    xn2_ref[...] = pltpu.bitcast(xn2, jnp.uint32)
    q = jnp.dot(xn2, wq_ref[...], preferred_element_type=F32).astype(BF16)
    qd = keys_ref.shape[2]
    for hp in range(keys_ref.shape[0]):
        st_ref[hp] = lax.dot_general(keys_ref[hp], q[:, hp * qd:(hp + 1) * qd],
                                     (((1,), (1,)), ((), ())), preferred_element_type=F32)


def _post(x, of, ob, h, ma, gn, wb, gbias_b, wo, g2, wq, keys, tm, col):
    T, D = x.shape
    nhp, nk, qd = keys.shape

    def tok(w, c=0):
        return pl.BlockSpec((tm, w), lambda i, c=c: (i, c))

    def full(shape):
        return pl.BlockSpec(shape, lambda i: (0,) * len(shape))

    return pl.pallas_call(
        _post_kernel,
        grid=(T // tm,),
        in_specs=[
            tok(D), tok(D), tok(D), tok(D, col["r"]), tok(D, col["gb"]), tok(D),
            full((1, D)), full((D, D)), full((1, D)), full((D, D)),
            full((1, D)), full(wq.shape), full(keys.shape),
        ],
        out_specs=[tok(D), pl.BlockSpec((tm // 2, D), lambda i: (i, 0)),
                   pl.BlockSpec((nhp, nk, tm), lambda i: (0, 0, i))],
        out_shape=[
            jax.ShapeDtypeStruct((T, D), F32),
            jax.ShapeDtypeStruct((T // 2, D), jnp.uint32),
            jax.ShapeDtypeStruct((nhp, nk, T), F32),
        ],
        compiler_params=_cparams(("parallel",)),
        name="post",
    )(x, of, ob, h, h, ma, gn, wb, gbias_b, wo, g2, wq, keys)


def _cand_rows():
    K = PEER_TOPK
    pieces = []
    for r1 in range(K // 2):
        lim = K // (r1 + 1)
        for start in range(0, lim, SUBLANES):
            pieces.append(("row", r1, start, min(lim - start, SUBLANES)))
    pieces.append(("col", K // 2, 0, SUBLANES))
    return pieces


def _extract_top(a, order, k, exact):
    big = float(a.shape[0]) if order is None else float(PEER_TOPK * PEER_TOPK)
    if exact and order is None:
        order = lax.broadcasted_iota(jnp.int32, a.shape, 0).astype(F32)
    rank = jnp.full(a.shape, float(k), F32)
    tops = []
    for r in range(k):
        m = jnp.max(a, axis=0, keepdims=True)
        sel = a == m
        if exact:
            first = jnp.min(jnp.where(sel, order, big), axis=0, keepdims=True)
            sel = order == first
        rank = jnp.where(sel, float(r), rank)
        a = jnp.where(sel, NEG_INF, a)
        tops.append(m)
    return rank, tops


def _pack_bf16(x):
    return pltpu.bitcast(x.astype(BF16), jnp.uint32)


def _unpack_bf16(words):
    return pltpu.bitcast(words, BF16)


def _stack_rows(rows, sub):
    out = jnp.broadcast_to(rows[0], sub.shape)
    for d in range(1, len(rows)):
        out = jnp.where(sub == d, rows[d], out)
    return out


def _topk_unit(s_ref, r2_ref, e2_ref, n1_ref, e1_ref, h, g, exact):
    K = PEER_TOPK
    lanes = slice(g * LANES, (g + 1) * LANES)
    pieces = _cand_rows()
    sub = lax.broadcasted_iota(jnp.int32, (SUBLANES, LANES), 0)
    s1 = s_ref[2 * h, :, lanes]
    s2 = s_ref[2 * h + 1, :, lanes]
    rank1, top1 = _extract_top(s1, None, K, exact)
    rank2, top2 = _extract_top(s2, None, K, exact)
    t1_hi = _stack_rows(top1[K // 2:], sub)
    t2_lo = _stack_rows(top2[:SUBLANES], sub)
    t2_hi = _stack_rows(top2[SUBLANES:], sub)
    cands, flats = [], []
    for kind, r1, start, cnt in pieces:
        if kind == "row":
            c = top1[r1] + (t2_lo if start == 0 else t2_hi)
            c = jnp.where(sub < cnt, c, NEG_INF)
            f = (r1 * K + start + sub).astype(F32)
        else:
            c = t1_hi + top2[0]
            f = ((r1 + sub) * K).astype(F32)
        cands.append(c)
        flats.append(f)
    cand = jnp.concatenate(cands, axis=0)
    flat = jnp.concatenate(flats, axis=0)
    cmax = top1[0] + top2[0]
    crank, _ = _extract_top(cand, flat, K, exact)
    taken = (crank < float(K)).astype(F32)
    z = jnp.sum(taken * jnp.exp(cand - cmax), axis=0, keepdims=True)
    n1 = jnp.zeros(s1.shape, F32)
    off = 0
    counts = {}
    for kind, r1, start, cnt in pieces:
        blk = taken[off:off + SUBLANES]
        off += SUBLANES
        if kind == "row":
            counts[r1] = counts.get(r1, 0.0) + jnp.sum(blk, axis=0, keepdims=True)
        else:
            for d in range(SUBLANES):
                counts[r1 + d] = blk[d:d + 1]
    for r1 in range(K):
        n1 = jnp.where(rank1 == float(r1), counts[r1], n1)
    r2_ref[h, g] = _pack_bf16(rank2)
    e2_ref[h, g] = _pack_bf16(jnp.exp(s2 - top2[0]) / z)
    n1_ref[h, :, lanes] = n1
    e1_ref[h, :, lanes] = jnp.exp(s1 - top1[0])

    def removed(rank):
        return jnp.sum((rank < float(K)).astype(F32), axis=0, keepdims=True)

    wrong = (removed(rank1) != float(K)) | (removed(rank2) != float(K)) | (removed(crank) != float(K))
    return wrong.astype(F32)


def _topk_kernel(s_ref, r2_ref, e2_ref, n1_ref, e1_ref):
    refs = (s_ref, r2_ref, e2_ref, n1_ref, e1_ref)

    groups = range(s_ref.shape[2] // LANES)

    def head(h, carry):
        suspect = [_topk_unit(*refs, h, g, exact=False) for g in groups]

        @pl.when(jnp.max(sum(suspect)) > 0.0)
        def _():
            for g in groups:
                _topk_unit(*refs, h, g, exact=True)
        return carry

    lax.fori_loop(0, PEER_HEADS, head, 0)


def _topk(st, tn):
    nhp, nk, T = st.shape

    def spec(rows):
        return pl.BlockSpec((PEER_HEADS, rows, tn), lambda i: (0, 0, i))

    slab_spec = pl.BlockSpec((PEER_HEADS, tn // LANES, nk // 2, LANES), lambda i: (0, i, 0, 0))
    slab = jax.ShapeDtypeStruct((PEER_HEADS, T // LANES, nk // 2, LANES), jnp.uint32)
    rows = jax.ShapeDtypeStruct((PEER_HEADS, nk, T), F32)
    return pl.pallas_call(
        _topk_kernel,
        grid=(T // tn,),
        in_specs=[pl.BlockSpec((nhp, nk, tn), lambda i: (0, 0, i))],
        out_specs=[slab_spec, slab_spec, spec(nk), spec(nk)],
        out_shape=[slab, slab, rows, rows],
        compiler_params=_cparams(("parallel",)),
        name="topk",
    )(st)


def _gelu(x):
    return 0.5 * x * (1.0 + lax.erf(x * (2.0 ** -0.5)))


def _peer_kernel(nj, xncur_ref, xnnext_ref, x2_ref, ucur_ref, unext_ref, vtprev_ref, vtcur_ref,
                 r2_ref, e2_ref, n1_ref, e1_ref, gf_ref, o_ref,
                 acc_ref, hid_a, hid_b, w_a, w_b):
    s = pl.program_id(0)
    j = s % nj
    nk = 2 * r2_ref.shape[2]
    half = ucur_ref.shape[0]
    per_half = half // nk

    tn = 2 * xncur_ref.shape[0]
    mm_cols = tn // 2

    def hidden(xn_ref, u_ref, urows, hid_ref, c):
        cols = slice(c * mm_cols, (c + 1) * mm_cols)
        xrows = slice(c * mm_cols // 2, (c + 1) * mm_cols // 2)
        hid_ref[:, cols] = lax.dot_general(_unpack_bf16(u_ref[urows, :]), _unpack_bf16(xn_ref[xrows, :]),
                                           (((1,), (1,)), ((), ())), preferred_element_type=F32)

    def accumulate(vt_ref, ecols, w_ref, c):
        cols = slice(c * mm_cols, (c + 1) * mm_cols)
        acc_ref[:, cols] += jnp.dot(_unpack_bf16(vt_ref[:, ecols]), _unpack_bf16(w_ref[:, cols]),
                                    preferred_element_type=F32)

    def weights(hid_ref, w_ref, i1, a):
        rows = slice(a * nk, (a + 1) * nk)
        wrows = slice(a * nk // 2, (a + 1) * nk // 2)
        pk = 2 * SUBLANES

        def row_tile(ref, h, cols):
            return jnp.broadcast_to(ref[h, i1:i1 + 1, cols], (pk, LANES)).astype(BF16)[None]

        for c in range(tn // LANES):
            cols = slice(c * LANES, (c + 1) * LANES)
            g = None
            for h in range(PEER_HEADS):
                n1 = row_tile(n1_ref, h, cols)
                e1 = row_tile(e1_ref, h, cols)
                r2 = _unpack_bf16(r2_ref[h, c]).reshape(nk // pk, pk, LANES)
                e2 = _unpack_bf16(e2_ref[h, c]).reshape(nk // pk, pk, LANES)
                t = jnp.where(r2 < n1, e2, jnp.zeros((), BF16)) * e1
                g = t if g is None else g + t
            act = _gelu(hid_ref[rows, cols]).astype(BF16)
            w_ref[wrows, cols] = pltpu.bitcast(act * g.reshape(nk, LANES), jnp.uint32)

    def half_step(hid_cur, w_cur, first_i1, xn_ref, u_ref, urows, hid_next, vt_ref, ecols, w_prev):
        assert per_half == 4
        hidden(xn_ref, u_ref, urows, hid_next, 0)
        weights(hid_cur, w_cur, first_i1, 0)
        accumulate(vt_ref, ecols, w_prev, 0)
        weights(hid_cur, w_cur, first_i1 + 1, 1)
        hidden(xn_ref, u_ref, urows, hid_next, 1)
        weights(hid_cur, w_cur, first_i1 + 2, 2)
        accumulate(vt_ref, ecols, w_prev, 1)
        weights(hid_cur, w_cur, first_i1 + 3, 3)

    lo, hi = slice(0, half), slice(half, 2 * half)
    ulo, uhi = slice(0, half // 2), slice(half // 2, half)

    @pl.when(s == 0)
    def _():
        hidden(xncur_ref, ucur_ref, ulo, hid_a, 0)
        hidden(xncur_ref, ucur_ref, ulo, hid_a, 1)
        w_b[...] = jnp.zeros_like(w_b)
        acc_ref[...] = jnp.zeros_like(acc_ref)

    half_step(hid_a, w_a, 0, xncur_ref, ucur_ref, uhi, hid_b, vtprev_ref, hi, w_b)

    @pl.when(jnp.logical_and(j == 0, s > 0))
    def _():
        x3 = x2_ref[...] + acc_ref[...].T
        o_ref[...] = _rms(x3, gf_ref[...])

    @pl.when(j == 0)
    def _():
        acc_ref[...] = jnp.zeros_like(acc_ref)

    half_step(hid_b, w_b, per_half, xnnext_ref, unext_ref, ulo, hid_a, vtcur_ref, lo, w_a)


def _pack_tables_kernel(u_ref, v_ref, up_ref, vtp_ref):
    up_ref[...] = _pack_bf16(u_ref[...])
    vtp_ref[...] = _pack_bf16(v_ref[...].T)


def _pack_tables(u, v, eb):
    E, D = u.shape
    return pl.pallas_call(
        _pack_tables_kernel,
        grid=(E // eb,),
        in_specs=[pl.BlockSpec((eb, D), lambda j: (j, 0)), pl.BlockSpec((eb, D), lambda j: (j, 0))],
        out_specs=[pl.BlockSpec((eb // 2, D), lambda j: (j, 0)),
                   pl.BlockSpec((D // 2, eb), lambda j: (0, j))],
        out_shape=[jax.ShapeDtypeStruct((E // 2, D), jnp.uint32),
                   jax.ShapeDtypeStruct((D // 2, E), jnp.uint32)],
        compiler_params=_cparams(("parallel",)),
        name="pack_tables",
    )(u, v)


def _peer(xn2, x2, u, vt, r2, e2, n1, e1, gf, tn, eb):
    T, D = x2.shape
    E = 2 * u.shape[0]
    ni, nj = T // tn, E // eb
    nsteps = ni * nj + 1

    def tile_of(s):
        return jnp.minimum(s // nj, ni - 1)

    def tab(t):
        return pl.BlockSpec((PEER_HEADS, tn // LANES) + t.shape[2:], lambda s: (0, tile_of(s), 0, 0))

    nk = n1.shape[1]
    per_blk = eb // nk
    n1 = n1.reshape(PEER_HEADS, nj, per_blk, T)
    e1 = e1.reshape(PEER_HEADS, nj, per_blk, T)
    row_tab = pl.BlockSpec((PEER_HEADS, None, per_blk, tn), lambda s: (0, s % nj, 0, tile_of(s)))
    done_tile = pl.BlockSpec((tn, D), lambda s: (jnp.maximum(s - 1, 0) // nj, 0))
    return pl.pallas_call(
        functools.partial(_peer_kernel, nj),
        grid=(nsteps,),
        in_specs=[
            pl.BlockSpec((tn // 2, D), lambda s: (tile_of(s), 0)),
            pl.BlockSpec((tn // 2, D), lambda s: (tile_of(s + 1), 0)),
            done_tile,
            pl.BlockSpec((eb // 2, D), lambda s: (s % nj, 0)),
            pl.BlockSpec((eb // 2, D), lambda s: ((s + 1) % nj, 0)),
            pl.BlockSpec((D // 2, eb), lambda s: (0, (s + nj - 1) % nj)),
            pl.BlockSpec((D // 2, eb), lambda s: (0, s % nj)),
            tab(r2), tab(e2), row_tab, row_tab,
            pl.BlockSpec((1, D), lambda s: (0, 0)),
        ],
        out_specs=done_tile,
        out_shape=jax.ShapeDtypeStruct((T, D), F32),
        scratch_shapes=[
            pltpu.VMEM((D, tn), F32),
            pltpu.VMEM((eb // 2, tn), F32),
            pltpu.VMEM((eb // 2, tn), F32),
            pltpu.VMEM((eb // 4, tn), jnp.uint32),
            pltpu.VMEM((eb // 4, tn), jnp.uint32),
        ],
        compiler_params=_cparams(("arbitrary",)),
        name="peer",
    )(xn2, xn2, x2, u, u, vt, vt, r2, e2, n1, e1, gf)


def _layer(x, norm1_g, w_in, conv_w, conv_b, wa_proj, decay_up_f, decay_bias_f, decay_up_b,
           decay_bias_b, gla_norm_g, wb_proj, gate_bias, w_out, norm2_g, peer_wq, peer_keys,
           peer_u, peer_v, out_g, batch, seq):
    T, D = x.shape
    dk_total = decay_up_f.shape[1]
    dv_total = wb_proj.shape[0]
    d_conv = conv_w.shape[1]
    assert d_conv == D and dv_total == D and 2 * dk_total == D

    sizes = (d_conv, d_conv, d_conv, dk_total, dk_total, dv_total, dv_total,
             GLA_LOWRANK, GLA_LOWRANK, D, D)
    offs = np.concatenate([[0], np.cumsum(sizes)])
    names = ("xa", "ba", "ca", "q", "k", "v", "r", "zf", "zb", "ga", "gb")
    seg = {n: (int(offs[i]), int(offs[i + 1])) for i, n in enumerate(names)}
    main = ("xa", "ba", "ca", "q", "k", "v", "r", "ga", "gb")
    w_main = jnp.concatenate([w_in[:, seg[n][0]:seg[n][1]] for n in main], axis=1).astype(BF16)
    wz = jnp.concatenate([w_in[:, seg["zf"][0]:seg["zb"][1]],
                          jnp.zeros((D, LANES - 2 * GLA_LOWRANK), w_in.dtype)], axis=1).astype(BF16)
    col, o = {}, 0
    for n in main:
        wdt = seg[n][1] - seg[n][0]
        col[n] = o // wdt
        o += wdt

    upf = jnp.zeros((LANES, dk_total), F32).at[:GLA_LOWRANK].set(decay_up_f)
    upb = jnp.zeros((LANES, dk_total), F32).at[GLA_LOWRANK:2 * GLA_LOWRANK].set(decay_up_b)

    h, z = _inproj(x, norm1_g.reshape(1, D), w_main, wz, tm=1024, tn=1024)
    ma = _conv_branch(h, conv_w, conv_b.reshape(1, D), wa_proj.astype(BF16),
                      gate_bias[0:1], seq, tm=512, col=col)
    of, ob = _gla(h, z, upf, decay_bias_f.reshape(1, -1), upb, decay_bias_b.reshape(1, -1),
                  batch, seq, tb=512, col=col, dk_total=dk_total, dv_total=dv_total)
    nh, _, nk, qd = peer_keys.shape
    keys = peer_keys.reshape(nh * 2, nk, qd).astype(BF16)
    x2, xn2, st = _post(x, of, ob, h, ma, gla_norm_g.reshape(1, -1), wb_proj.astype(BF16),
                        gate_bias[1:2], w_out.astype(BF16), norm2_g.reshape(1, D),
                        peer_wq.astype(BF16), keys, tm=256, col=col)
    r2, e2, n1, e1 = _topk(st, tn=256)
    up, vtp = _pack_tables(peer_u, peer_v, eb=512)
    return _peer(xn2, x2, up, vtp, r2, e2, n1, e1, out_g.reshape(1, D), tn=512, eb=1024)


def kernel(x, norm1_g, w_in, conv_w, conv_b, wa_proj, decay_up_f, decay_bias_f, decay_up_b,
           decay_bias_b, gla_norm_g, wb_proj, gate_bias, w_out, norm2_g, peer_wq, peer_keys,
           peer_u, peer_v, final_norm_g):
    batch, seq, D = x.shape
    depth = w_in.shape[0]
    assert depth == 1, "the final norm is fused into the last layer's PEER kernel"
    y = _layer(x.reshape(batch * seq, D), norm1_g[0], w_in[0], conv_w[0], conv_b[0], wa_proj[0],
               decay_up_f[0], decay_bias_f[0], decay_up_b[0], decay_bias_b[0], gla_norm_g[0],
               wb_proj[0], gate_bias[0], w_out[0], norm2_g[0], peer_wq[0], peer_keys[0],
               peer_u[0], peer_v[0], final_norm_g, batch, seq)
    return y.reshape(batch, seq, D)
```
